```python
import jax
import jax.numpy as jnp
from jax import lax
import numpy as np

D_MODEL = 1024
BATCH = 8
SEQ = 8192
DEPTH = 2

HEAD_DIM = 64
MIX_WIDTH = D_MODEL
A_WIDTH = D_MODEL // 4
A_HEADS = A_WIDTH // HEAD_DIM
A_DK = HEAD_DIM
A_DV = HEAD_DIM
B_WIDTH = D_MODEL // 2
B_BLOCKS = B_WIDTH // HEAD_DIM
B_CONV = 4
LRU_C = 8.0
C_WIDTH = MIX_WIDTH - A_WIDTH - B_WIDTH
C_HEADS = C_WIDTH // HEAD_DIM
C_DV = HEAD_DIM
C_DK = C_DV // 2
C_GATE_RANK = 16
C_GATE_NORM = 16.0
CHUNK = 64
D_FF = 2688
FFN_CONV = 3
EPS = 1e-6
F_FLOOR = 1e-20

IN_SIZES = (A_HEADS * A_DK, A_HEADS * A_DK, A_HEADS * A_DV, A_HEADS * A_DV,
            B_WIDTH, B_WIDTH,
            C_HEADS * C_DK, C_HEADS * C_DK, C_WIDTH, C_WIDTH, C_GATE_RANK)
IN_SPLITS = tuple(int(s) for s in np.cumsum(IN_SIZES)[:-1])
IN_COLS = int(sum(IN_SIZES))

kernel_name = 'hymba_style_hgrn2_rglru_gla_convglu'


def rmsnorm(x, w):
    xf = x.astype(jnp.float32)
    y = xf * lax.rsqrt(jnp.mean(xf * xf, axis=-1, keepdims=True) + EPS)
    return (y * w.astype(jnp.float32)).astype(x.dtype)


def head_rmsnorm(o, w):
    b, t, h, d = o.shape
    return rmsnorm(o, w.reshape(h, d)).reshape(b, t, h * d)


def causal_dwconv(x, w, bias):
    k_w = w.shape[0]
    t = x.shape[1]
    xp = jnp.pad(x, ((0, 0), (k_w - 1, 0), (0, 0)))
    y = bias + xp[:, k_w - 1:k_w - 1 + t] * w[k_w - 1]
    for k in range(k_w - 1):
        y = y + xp[:, k:k + t] * w[k]
    return y


def chunked_gated_linear_attention(q, k, v, log_f):
    b_sz, t, h, dk = q.shape
    dv = v.shape[-1]
    n = t // CHUNK

    def to_chunks(a):
        return a.astype(jnp.float32).reshape(b_sz, n, CHUNK, h, a.shape[-1]).transpose(1, 0, 3, 2, 4)

    qc, kc, vc, gc = to_chunks(q), to_chunks(k), to_chunks(v), to_chunks(log_f)
    causal = jnp.tril(jnp.ones((CHUNK, CHUNK), dtype=bool))[:, :, None]

    def step(state, inp):
        qi, ki, vi, gi = inp
        cum = jnp.cumsum(gi, axis=2)
        diff = cum[:, :, :, None, :] - cum[:, :, None, :, :]
        decay = jnp.where(causal, jnp.exp(jnp.where(causal, diff, 0.0)), 0.0)
        scores = jnp.einsum('bhid,bhijd,bhjd->bhij', qi, decay, ki)
        last = cum[:, :, -1:, :]
        out = (jnp.einsum('bhij,bhjv->bhiv', scores, vi)
               + jnp.einsum('bhid,bhdv->bhiv', qi * jnp.exp(cum), state))
        state = (jnp.exp(last[:, :, 0, :, None]) * state
                 + jnp.einsum('bhjd,bhjv->bhdv', ki * jnp.exp(last - cum), vi))
        return state, out

    s0 = jnp.zeros((b_sz, h, dk, dv), jnp.float32)
    _, o = lax.scan(step, s0, (qc, kc, vc, gc))
    return o.transpose(1, 0, 3, 2, 4).reshape(b_sz, t, h, dv).astype(q.dtype)


def hgrn2_mixer(q, f_logit, i, g, lower_bound, norm_w):
    b_sz, t, _ = q.shape
    lb = lower_bound.astype(jnp.float32)
    z = f_logit.astype(jnp.float32)
    f = lb + (1.0 - lb) * jax.nn.sigmoid(z)
    log_f = jnp.log(jnp.maximum(f, F_FLOOR))
    key = (1.0 - lb) * jax.nn.sigmoid(-z)
    heads = lambda a, d: a.reshape(b_sz, t, A_HEADS, d)
    o = chunked_gated_linear_attention(heads(q, A_DK), heads(key, A_DK), heads(i, A_DV), heads(log_f, A_DK))
    return head_rmsnorm(o, norm_w) * jax.nn.silu(g)


def rglru_mixer(xb, gb, conv_w, conv_b, w_a, b_a, w_x, b_x, lam):
    b_sz, t, _ = xb.shape
    xc = causal_dwconv(xb, conv_w, conv_b)
    xh = xc.reshape(b_sz, t, B_BLOCKS, HEAD_DIM)
    r = jax.nn.sigmoid(jnp.einsum('bthi,hij->bthj', xh, w_a).reshape(b_sz, t, B_WIDTH) + b_a)
    gi = jax.nn.sigmoid(jnp.einsum('bthi,hij->bthj', xh, w_x).reshape(b_sz, t, B_WIDTH) + b_x)
    log_a = (-LRU_C * r.astype(jnp.float32)) * jax.nn.softplus(-lam.astype(jnp.float32))
    a = jnp.exp(log_a)
    u = jnp.sqrt(jnp.maximum(-jnp.expm1(2.0 * log_a), 0.0)) * (gi * xc).astype(jnp.float32)

    def combine(left, right):
        a1, u1 = left
        a2, u2 = right
        return a1 * a2, a2 * u1 + u2

    _, hs = lax.associative_scan(combine, (a, u), axis=1)
    return hs.astype(xb.dtype) * jax.nn.gelu(gb)


def gla_mixer(q, k, v, g, g_lr, w_g_up, b_g, norm_w):
    b_sz, t, _ = q.shape
    log_f = jax.nn.log_sigmoid((g_lr @ w_g_up + b_g).astype(jnp.float32)) / C_GATE_NORM
    qh = q.reshape(b_sz, t, C_HEADS, C_DK) * (C_DK ** -0.5)
    kh = k.reshape(b_sz, t, C_HEADS, C_DK)
    vh = v.reshape(b_sz, t, C_HEADS, C_DV)
    o = chunked_gated_linear_attention(qh, kh, vh, log_f.reshape(b_sz, t, C_HEADS, C_DK))
    return head_rmsnorm(o, norm_w) * jax.nn.silu(g)


def setup_inputs(seed: int = 0) -> dict:
    key = jax.random.key(seed)
    ks = jax.random.split(key, 24)
    f32 = jnp.float32

    def nrm(k, shape, scale):
        return scale * jax.random.normal(k, shape, f32)

    def gain(k, shape):
        return 1.0 + 0.05 * jax.random.normal(k, shape, f32)

    a0 = jax.random.uniform(ks[11], (DEPTH, B_WIDTH), f32, 0.9, 0.999)
    sig = a0 ** (1.0 / LRU_C)
    lam = jnp.log(sig) - jnp.log1p(-sig)
    return {
        'x': jax.random.normal(ks[0], (BATCH, SEQ, D_MODEL), f32),
        'norm_mix': gain(ks[1], (DEPTH, D_MODEL)),
        'w_in': nrm(ks[2], (DEPTH, D_MODEL, IN_COLS), D_MODEL ** -0.5),
        'hgrn_lower_bounds': nrm(ks[3], (DEPTH, A_HEADS * A_DK), 0.5),
        'hgrn_norm': gain(ks[4], (DEPTH, A_HEADS * A_DV)),
        'rg_conv_w': nrm(ks[5], (DEPTH, B_CONV, B_WIDTH), B_CONV ** -0.5),
        'rg_conv_b': nrm(ks[6], (DEPTH, B_WIDTH), 0.05),
        'rg_w_a': nrm(ks[7], (DEPTH, B_BLOCKS, HEAD_DIM, HEAD_DIM), HEAD_DIM ** -0.5),
        'rg_b_a': nrm(ks[8], (DEPTH, B_WIDTH), 0.1),
        'rg_w_x': nrm(ks[9], (DEPTH, B_BLOCKS, HEAD_DIM, HEAD_DIM), HEAD_DIM ** -0.5),
        'rg_b_x': nrm(ks[10], (DEPTH, B_WIDTH), 0.1),
        'rg_lambda': lam,
        'gla_w_gate_up': nrm(ks[12], (DEPTH, C_GATE_RANK, C_HEADS * C_DK), C_GATE_RANK ** -0.5),
        'gla_b_gate': nrm(ks[13], (DEPTH, C_HEADS * C_DK), 0.1),
        'gla_norm': gain(ks[14], (DEPTH, C_WIDTH)),
        'w_out': nrm(ks[15], (DEPTH, MIX_WIDTH, D_MODEL), MIX_WIDTH ** -0.5),
        'norm_ffn': gain(ks[16], (DEPTH, D_MODEL)),
        'ffn_w_gate': nrm(ks[17], (DEPTH, D_MODEL, D_FF), D_MODEL ** -0.5),
        'ffn_w_val': nrm(ks[18], (DEPTH, D_MODEL, D_FF), D_MODEL ** -0.5),
        'ffn_conv_w': nrm(ks[19], (DEPTH, FFN_CONV, D_FF), FFN_CONV ** -0.5),
        'ffn_conv_b': nrm(ks[20], (DEPTH, D_FF), 0.05),
        'ffn_w_down': nrm(ks[21], (DEPTH, D_FF, D_MODEL), D_FF ** -0.5),
        'norm_final': gain(ks[22], (D_MODEL,)),
    }


def reference(x, norm_mix, w_in, hgrn_lower_bounds, hgrn_norm, rg_conv_w, rg_conv_b, rg_w_a, rg_b_a,
              rg_w_x, rg_b_x, rg_lambda, gla_w_gate_up, gla_b_gate, gla_norm, w_out, norm_ffn,
              ffn_w_gate, ffn_w_val, ffn_conv_w, ffn_conv_b, ffn_w_down, norm_final):
    lb = jax.nn.softmax(hgrn_lower_bounds.astype(jnp.float32), axis=0)
    lb = jnp.cumsum(lb, axis=0) - lb[0]
    for l in range(DEPTH):
        h = rmsnorm(x, norm_mix[l])
        proj = h @ w_in[l]
        a_q, a_f, a_i, a_g, b_x, b_g, c_q, c_k, c_v, c_g, c_lr = jnp.split(proj, IN_SPLITS, axis=-1)
        o_a = hgrn2_mixer(a_q, a_f, a_i, a_g, lb[l], hgrn_norm[l])
        o_b = rglru_mixer(b_x, b_g, rg_conv_w[l], rg_conv_b[l], rg_w_a[l], rg_b_a[l],
                          rg_w_x[l], rg_b_x[l], rg_lambda[l])
        o_c = gla_mixer(c_q, c_k, c_v, c_g, c_lr, gla_w_gate_up[l], gla_b_gate[l], gla_norm[l])
        x = x + jnp.concatenate([o_a, o_b, o_c], axis=-1) @ w_out[l]
        h = rmsnorm(x, norm_ffn[l])
        gate = causal_dwconv(h @ ffn_w_gate[l], ffn_conv_w[l], ffn_conv_b[l])
        x = x + (jax.nn.gelu(gate) * (h @ ffn_w_val[l])) @ ffn_w_down[l]
    return rmsnorm(x, norm_final)
```

```python
import functools

import jax
import jax.numpy as jnp
from jax import lax
from jax.experimental import pallas as pl
from jax.experimental.pallas import tpu as pltpu

D_MODEL = 1024
HEAD_DIM = 64
N_HEADS = 4
A_DK = 64
C_DK = 32
B_WIDTH = 512
B_CONV = 4
LRU_C = 8.0
C_GATE_RANK = 16
C_GATE_NORM = 16.0
D_FF = 2688
FFN_CONV = 3
EPS = 1e-6
F_FLOOR = 1e-20
V_WIDTH = N_HEADS * HEAD_DIM

OFF_AQ, OFF_AF, OFF_AI, OFF_AG = 0, 256, 512, 768
OFF_BX, OFF_BG = 1024, 1536
OFF_CQ, OFF_CK, OFF_CV, OFF_CG, OFF_CLR = 2048, 2176, 2304, 2560, 2816
IN_COLS = 2832
LANES = 128
SUBLANES = 8
IN_COLS_PAD = 2944

CHUNK = 64
MIX_ROWS = 256
FFN_ROWS = 512
VMEM_LIMIT = 52 * 1024 * 1024

BF16 = jnp.bfloat16
F32 = jnp.float32


def _dot(a, b):
    return jnp.dot(a, b, preferred_element_type=F32)


def _dot_nt(a, b):
    return lax.dot_general(a, b, (((1,), (1,)), ((), ())), preferred_element_type=F32)


def _dot_tn(a, b):
    return lax.dot_general(a, b, (((0,), (0,)), ((), ())), preferred_element_type=F32)


def _rmsnorm(x, w):
    return x * lax.rsqrt(jnp.mean(x * x, axis=-1, keepdims=True) + EPS) * w


def _softplus(x):
    return jnp.maximum(x, 0.0) + jnp.log1p(jnp.exp(-jnp.abs(x)))


def _expm1(x):
    u = jnp.exp(x)
    near = x > -1.0
    un = jnp.where(near, u, 0.5)
    lu = jnp.log(un)
    flat = lu == 0.0
    kahan = jnp.where(flat, x, (un - 1.0) * x / jnp.where(flat, 1.0, lu))
    return jnp.where(near, kahan, u - 1.0)


def _row_ids(shape):
    return lax.broadcasted_iota(jnp.int32, shape, 0)


def _bcast_block_row(x, blk, r):
    n, w = x.shape
    if blk >= SUBLANES:
        row = x.reshape(n // blk, blk, w)[:, r:r + 1, :]
        return jnp.broadcast_to(row, (n // blk, blk, w)).reshape(n, w)
    rid = _row_ids((n, w)) & (blk - 1)
    out = x
    for m in range(blk):
        if m != r:
            out = jnp.where(rid == m, pltpu.roll(x, (m - r) % n, 0), out)
    return out


def _split_bf16(y):
    hi = y.astype(BF16)
    lo = (y - hi.astype(F32)).astype(BF16)
    return hi, lo


def _head_rmsnorm(o, w, seg_ones):
    hi, lo = _split_bf16(o * o)
    ms = (_dot(hi, seg_ones) + _dot(lo, seg_ones)) * (1.0 / HEAD_DIM)
    return o * lax.rsqrt(ms + EPS) * w


def _linear_attention(q, k, v, g, st_ref, dk):
    rows, hk = q.shape
    row = _row_ids((rows, hk))

    levels = [(q.astype(BF16), k.astype(BF16))]
    w = g
    span = 1
    while span < CHUNK:
        mid = _bcast_block_row(w, 2 * span, span - 1)
        right = (row & span) != 0
        qh = jnp.where(right, q * jnp.exp(jnp.where(right, w, 0.0)), 0.0)
        kh = jnp.where(right, 0.0, k * jnp.exp(jnp.where(right, 0.0, mid - w)))
        levels.append((qh.astype(BF16), kh.astype(BF16)))
        w = w + jnp.where(right, mid, 0.0)
        span *= 2
    tot = _bcast_block_row(w, CHUNK, CHUNK - 1)
    q_in = (q * jnp.exp(w)).astype(BF16)
    k_out = (k * jnp.exp(tot - w)).astype(BF16)
    v_bf = v.astype(BF16)

    ii = _row_ids((CHUNK, V_WIDTH))
    jj = lax.broadcasted_iota(jnp.int32, (CHUNK, V_WIDTH), 1) & (CHUNK - 1)
    level_masks = [ii == jj]
    span = 1
    while span < CHUNK:
        shift = span.bit_length()
        level_masks.append((ii >> shift) == (jj >> shift))
        span *= 2
    kr = _row_ids((V_WIDTH, hk)) // CHUNK
    kc = lax.broadcasted_iota(jnp.int32, (V_WIDTH, hk), 1) // dk
    k_sel = kr == kc
    vr = _row_ids((V_WIDTH, V_WIDTH)) // CHUNK
    vc = lax.broadcasted_iota(jnp.int32, (V_WIDTH, V_WIDTH), 1) // HEAD_DIM
    v_sel = vr == vc
    sr = _row_ids((V_WIDTH, hk)) // HEAD_DIM
    s_sel = sr == kc

    outs = []
    for c in range(rows // CHUNK):
        sl = slice(c * CHUNK, (c + 1) * CHUNK)
        scores = None
        for (qh, kh), msk in zip(reversed(levels), reversed(level_masks)):
            k_rep = jnp.where(k_sel, jnp.concatenate([kh[sl]] * N_HEADS, axis=0), 0.0)
            s = _dot_nt(qh[sl], k_rep)
            scores = s if scores is None else jnp.where(msk, s, scores)
        v_rep = jnp.where(v_sel, jnp.concatenate([v_bf[sl]] * N_HEADS, axis=0), 0.0)
        o = _dot(scores.astype(BF16), v_rep)
        st = st_ref[...]
        o = o + _dot_nt(q_in[sl], st.astype(BF16))
        upd = jnp.where(s_sel, _dot_tn(v_bf[sl], k_out[sl]), 0.0)
        st_ref[...] = st * jnp.exp(tot[c * CHUNK:c * CHUNK + 1, :]) + upd
        outs.append(o)
    return jnp.concatenate(outs, axis=0)


def _linear_recurrence(a, u, carry_ref):
    rows, width = a.shape
    row = _row_ids((rows, width))
    p, h = a, u
    span = 1
    while span < SUBLANES:
        right = (row & span) != 0
        pm = _bcast_block_row(p, 2 * span, span - 1)
        hm = _bcast_block_row(h, 2 * span, span - 1)
        h = jnp.where(right, h + p * hm, h)
        p = jnp.where(right, p * pm, p)
        span *= 2
    carry = carry_ref[...]
    outs = []
    for grp in range(rows // SUBLANES):
        sl = slice(grp * SUBLANES, (grp + 1) * SUBLANES)
        hg = h[sl] + p[sl] * carry
        carry = hg[SUBLANES - 1:SUBLANES, :]
        outs.append(hg)
    carry_ref[...] = carry
    return jnp.concatenate(outs, axis=0)


def _mixer_kernel(layer, x_ref, nmix_ref, win_ref, lbraw_ref, anorm_ref, convw_ref, convb_ref,
                  wa_ref, ba_ref, wx_ref, bx_ref, lam_ref, wup_ref, bg_ref, cnorm_ref, wout_ref,
                  out_ref, sta_ref, stc_ref, hb_ref, xbext_ref):
    rows = x_ref.shape[0]

    @pl.when(pl.program_id(1) == 0)
    def _():
        sta_ref[...] = jnp.zeros_like(sta_ref)
        stc_ref[...] = jnp.zeros_like(stc_ref)
        hb_ref[...] = jnp.zeros_like(hb_ref)
        xbext_ref[0:SUBLANES, :] = jnp.zeros((SUBLANES, B_WIDTH), F32)

    x = x_ref[...]
    h = _rmsnorm(x, nmix_ref[...]).astype(BF16)
    proj = _dot(h, win_ref[...])

    seg_ones = (_row_ids((V_WIDTH, V_WIDTH)) // HEAD_DIM ==
                lax.broadcasted_iota(jnp.int32, (V_WIDTH, V_WIDTH), 1) // HEAD_DIM).astype(BF16)

    lbraw = lbraw_ref[...]
    e = jnp.exp(lbraw - jnp.max(lbraw, axis=0, keepdims=True))
    sm = e / jnp.sum(e, axis=0, keepdims=True)
    lb = jnp.sum(sm[0:layer + 1], axis=0, keepdims=True) - sm[0:1]
    z = proj[:, OFF_AF:OFF_AF + 256]
    f = lb + (1.0 - lb) * jax.nn.sigmoid(z)
    log_f = jnp.log(jnp.maximum(f, F_FLOOR))
    key = (1.0 - lb) * jax.nn.sigmoid(-z)
    o_a = _linear_attention(proj[:, OFF_AQ:OFF_AQ + 256], key, proj[:, OFF_AI:OFF_AI + 256],
                            log_f, sta_ref, A_DK)
    o_a = _head_rmsnorm(o_a, anorm_ref[...], seg_ones) * jax.nn.silu(proj[:, OFF_AG:OFF_AG + 256])

    xb = proj[:, OFF_BX:OFF_BX + B_WIDTH]
    xbext_ref[SUBLANES:SUBLANES + rows, :] = xb
    convw = convw_ref[...]
    xc = convb_ref[...] + xb * convw[B_CONV - 1:B_CONV, :]
    for kk in range(B_CONV - 1):
        back = B_CONV - 1 - kk
        xc = xc + xbext_ref[pl.ds(SUBLANES - back, rows), :] * convw[kk:kk + 1, :]
    xbext_ref[0:SUBLANES, :] = xbext_ref[rows:rows + SUBLANES, :]
    xc_bf = xc.astype(BF16)
    r = jax.nn.sigmoid(_dot(xc_bf, wa_ref[...]) + ba_ref[...])
    gi = jax.nn.sigmoid(_dot(xc_bf, wx_ref[...]) + bx_ref[...])
    log_a = (-LRU_C * r) * _softplus(-lam_ref[...])
    a = jnp.exp(log_a)
    u = jnp.sqrt(jnp.maximum(-_expm1(2.0 * log_a), 0.0)) * (gi * xc)
    hs = _linear_recurrence(a, u, hb_ref)
    o_b = hs * jax.nn.gelu(proj[:, OFF_BG:OFF_BG + B_WIDTH])

    glr = proj[:, OFF_CLR:OFF_CLR + LANES].astype(BF16)
    gate = _dot(glr, wup_ref[...]) + bg_ref[...]
    log_fc = -_softplus(-gate) / C_GATE_NORM
    o_c = _linear_attention(proj[:, OFF_CQ:OFF_CQ + 128] * (C_DK ** -0.5),
                            proj[:, OFF_CK:OFF_CK + 128], proj[:, OFF_CV:OFF_CV + 256],
                            log_fc, stc_ref, C_DK)
    o_c = _head_rmsnorm(o_c, cnorm_ref[...], seg_ones) * jax.nn.silu(proj[:, OFF_CG:OFF_CG + 256])

    mix = jnp.concatenate([o_a, o_b, o_c], axis=-1).astype(BF16)
    out_ref[...] = x + _dot(mix, wout_ref[...])


def _ffn_kernel(final, x_ref, nffn_ref, wg_ref, wv_ref, convw_ref, convb_ref, wd_ref, nfin_ref,
                out_ref, gext_ref):
    rows = x_ref.shape[0]

    @pl.when(pl.program_id(1) == 0)
    def _():
        gext_ref[0:SUBLANES, :] = jnp.zeros((SUBLANES, D_FF), F32)

    x = x_ref[...]
    h = _rmsnorm(x, nffn_ref[...]).astype(BF16)
    pre = _dot(h, wg_ref[...])
    gext_ref[SUBLANES:SUBLANES + rows, :] = pre
    convw = convw_ref[...]
    gate = convb_ref[...] + pre * convw[FFN_CONV - 1:FFN_CONV, :]
    for kk in range(FFN_CONV - 1):
        back = FFN_CONV - 1 - kk
        gate = gate + gext_ref[pl.ds(SUBLANES - back, rows), :] * convw[kk:kk + 1, :]
    gext_ref[0:SUBLANES, :] = gext_ref[rows:rows + SUBLANES, :]
    val = _dot(h, wv_ref[...])
    act = (jax.nn.gelu(gate) * val).astype(BF16)
    y = x + _dot(act, wd_ref[...])
    if final:
        y = _rmsnorm(y, nfin_ref[...])
    out_ref[...] = y


def _const_spec(shape):
    return pl.BlockSpec(shape, lambda b, t: (0,) * len(shape), pipeline_mode=pl.Buffered(1))


def _tile_spec(rows):
    return pl.BlockSpec((None, rows, D_MODEL), lambda b, t: (b, t, 0))


def _compiler_params():
    return pltpu.CompilerParams(dimension_semantics=("arbitrary", "arbitrary"),
                                vmem_limit_bytes=VMEM_LIMIT)


def _mixer_call(layer, x, consts):
    batch, seq, _ = x.shape
    rows = min(MIX_ROWS, seq)
    return pl.pallas_call(
        functools.partial(_mixer_kernel, layer),
        grid=(batch, seq // rows),
        in_specs=[_tile_spec(rows)] + [_const_spec(c.shape) for c in consts],
        out_specs=_tile_spec(rows),
        out_shape=jax.ShapeDtypeStruct(x.shape, F32),
        scratch_shapes=[pltpu.VMEM((V_WIDTH, N_HEADS * A_DK), F32),
                        pltpu.VMEM((V_WIDTH, N_HEADS * C_DK), F32),
                        pltpu.VMEM((1, B_WIDTH), F32),
                        pltpu.VMEM((rows + SUBLANES, B_WIDTH), F32)],
        compiler_params=_compiler_params(),
        name=f"mixer{layer}",
    )(x, *consts)


def _ffn_call(layer, final, x, consts):
    batch, seq, _ = x.shape
    rows = min(FFN_ROWS, seq)
    return pl.pallas_call(
        functools.partial(_ffn_kernel, final),
        grid=(batch, seq // rows),
        in_specs=[_tile_spec(rows)] + [_const_spec(c.shape) for c in consts],
        out_specs=_tile_spec(rows),
        out_shape=jax.ShapeDtypeStruct(x.shape, F32),
        scratch_shapes=[pltpu.VMEM((rows + SUBLANES, D_FF), F32)],
        compiler_params=_compiler_params(),
        name=f"ffn{layer}",
    )(x, *consts)


def _block_diag(w):
    nb, n, _ = w.shape
    eye = jnp.eye(nb, dtype=w.dtype)
    return (eye[:, None, :, None] * w[:, :, None, :]).reshape(nb * n, nb * n)


def kernel(x, norm_mix, w_in, hgrn_lower_bounds, hgrn_norm, rg_conv_w, rg_conv_b, rg_w_a, rg_b_a, rg_w_x, rg_b_x, rg_lambda, gla_w_gate_up, gla_b_gate, gla_norm, w_out, norm_ffn, ffn_w_gate, ffn_w_val, ffn_conv_w, ffn_conv_b, ffn_w_down, norm_final):
    depth = w_in.shape[0]
    row = lambda v: v.reshape(1, -1).astype(F32)
    for l in range(depth):
        w_in_pad = jnp.pad(w_in[l], ((0, 0), (0, IN_COLS_PAD - IN_COLS))).astype(BF16)
        w_up_pad = jnp.pad(gla_w_gate_up[l], ((0, LANES - C_GATE_RANK), (0, 0))).astype(BF16)
        mixer_consts = [
            row(norm_mix[l]), w_in_pad, hgrn_lower_bounds.astype(F32), row(hgrn_norm[l]),
            rg_conv_w[l].astype(F32), row(rg_conv_b[l]),
            _block_diag(rg_w_a[l]).astype(BF16), row(rg_b_a[l]),
            _block_diag(rg_w_x[l]).astype(BF16), row(rg_b_x[l]), row(rg_lambda[l]),
            w_up_pad, row(gla_b_gate[l]), row(gla_norm[l]), w_out[l].astype(BF16),
        ]
        x = _mixer_call(l, x, mixer_consts)
        ffn_consts = [
            row(norm_ffn[l]), ffn_w_gate[l].astype(BF16), ffn_w_val[l].astype(BF16),
            ffn_conv_w[l].astype(F32), row(ffn_conv_b[l]), ffn_w_down[l].astype(BF16),
            row(norm_final),
        ]
        x = _ffn_call(l, l == depth - 1, x, ffn_consts)
    return x
```

```python
import functools

import numpy as np
import jax
import jax.numpy as jnp
from jax import lax
from jax.experimental import pallas as pl
from jax.experimental.pallas import tpu as pltpu

D_MODEL = 1024
HEAD_DIM = 64
N_HEADS = 4
A_DK = 64
C_DK = 32
B_WIDTH = 512
B_CONV = 4
LRU_C = 8.0
C_GATE_RANK = 16
C_GATE_NORM = 16.0
D_FF = 2688
FFN_CONV = 3
EPS = 1e-6
F_FLOOR = 1e-20
V_WIDTH = N_HEADS * HEAD_DIM

OFF_AQ, OFF_AF, OFF_AI, OFF_AG = 0, 256, 512, 768
OFF_BX, OFF_BG = 1024, 1536
OFF_CQ, OFF_CK, OFF_CV, OFF_CG, OFF_CLR = 2048, 2176, 2304, 2560, 2816
IN_COLS = 2832
LANES = 128
SUBLANES = 8
IN_COLS_PAD = 2944

CHUNK = 64
MIX_ROWS = 256
FFN_ROWS = 512
VMEM_LIMIT = 52 * 1024 * 1024
FAST_BOUND = 60.0

BF16 = jnp.bfloat16
F32 = jnp.float32


def _dot(a, b):
    return jnp.dot(a, b, preferred_element_type=F32)


def _dot_nt(a, b):
    return lax.dot_general(a, b, (((1,), (1,)), ((), ())), preferred_element_type=F32)


def _dot_tn(a, b):
    return lax.dot_general(a, b, (((0,), (0,)), ((), ())), preferred_element_type=F32)


def _rmsnorm(x, w):
    return x * lax.rsqrt(jnp.mean(x * x, axis=-1, keepdims=True) + EPS) * w


def _softplus(x):
    return jnp.maximum(x, 0.0) + jnp.log1p(jnp.exp(-jnp.abs(x)))


def _row_ids(shape):
    return lax.broadcasted_iota(jnp.int32, shape, 0)


def _bcast_block_row(x, blk, r):
    n, w = x.shape
    if blk >= SUBLANES:
        row = x.reshape(n // blk, blk, w)[:, r:r + 1, :]
        return jnp.broadcast_to(row, (n // blk, blk, w)).reshape(n, w)
    rid = _row_ids((n, w)) & (blk - 1)
    out = x
    for m in range(blk):
        if m != r:
            out = jnp.where(rid == m, pltpu.roll(x, (m - r) % n, 0), out)
    return out


def _split2(y):
    hi = y.astype(BF16)
    lo = (y - hi.astype(F32)).astype(BF16)
    return hi, lo


def _split3(y):
    hi = y.astype(BF16)
    r1 = y - hi.astype(F32)
    mid = r1.astype(BF16)
    lo = (r1 - mid.astype(F32)).astype(BF16)
    return hi, mid, lo


def _head_rmsnorm(o, w, seg_ones):
    hi, lo = _split2(o * o)
    ms = (_dot(hi, seg_ones) + _dot(lo, seg_ones)) * (1.0 / HEAD_DIM)
    return o * lax.rsqrt(ms + EPS) * w


def _la_chunks(levels, level_masks, q_in, k_out, v_bf, tot, sel, seg_ones, st):
    rows = v_bf.shape[0]
    outs = []
    for c in range(rows // CHUNK):
        sl = slice(c * CHUNK, (c + 1) * CHUNK)
        scores = None
        for (qh, kh), msk in zip(levels, level_masks):
            k_rep = jnp.concatenate([kh[sl]] * N_HEADS, axis=0) * sel
            s = _dot_nt(qh[sl], k_rep)
            scores = jnp.where(msk, s, 0.0 if scores is None else scores)
        v_rep = jnp.concatenate([v_bf[sl]] * N_HEADS, axis=0) * seg_ones
        o = _dot(scores.astype(BF16), v_rep)
        outs.append(o + _dot_nt(q_in[sl], st.astype(BF16) * sel))
        st = st * jnp.exp(tot[c * CHUNK:c * CHUNK + 1, :]) + _dot_tn(v_bf[sl], k_out[sl])
    return jnp.concatenate(outs, axis=0), st


def _chunk_ids():
    ii = _row_ids((CHUNK, V_WIDTH))
    jj = lax.broadcasted_iota(jnp.int32, (CHUNK, V_WIDTH), 1) & (CHUNK - 1)
    return ii, jj


def _la_fast(q, k, v_bf, w, d, sel, seg_ones, st):
    tot = _bcast_block_row(w, CHUNK, CHUNK - 1)
    q_hat = (q * jnp.exp(d)).astype(BF16)
    k_hat = (k * jnp.exp(-d)).astype(BF16)
    q_in = (q * jnp.exp(w)).astype(BF16)
    k_out = (k * jnp.exp(tot - w)).astype(BF16)
    ii, jj = _chunk_ids()
    return _la_chunks([(q_hat, k_hat)], [jj <= ii], q_in, k_out, v_bf, tot, sel, seg_ones, st)


def _la_safe(q, k, v_bf, g, sel, seg_ones, st):
    rows, hk = q.shape
    row = _row_ids((rows, hk))
    ii, jj = _chunk_ids()
    levels = [(q.astype(BF16), k.astype(BF16))]
    level_masks = [ii == jj]
    w = g
    span = 1
    while span < CHUNK:
        mid = _bcast_block_row(w, 2 * span, span - 1)
        right = (row & span) != 0
        qh = jnp.where(right, q * jnp.exp(jnp.where(right, w, 0.0)), 0.0)
        kh = jnp.where(right, 0.0, k * jnp.exp(jnp.where(right, 0.0, mid - w)))
        levels.append((qh.astype(BF16), kh.astype(BF16)))
        shift = span.bit_length()
        level_masks.append((ii >> shift) == (jj >> shift))
        w = w + jnp.where(right, mid, 0.0)
        span *= 2
    tot = _bcast_block_row(w, CHUNK, CHUNK - 1)
    q_in = (q * jnp.exp(w)).astype(BF16)
    k_out = (k * jnp.exp(tot - w)).astype(BF16)
    return _la_chunks(levels[::-1], level_masks[::-1], q_in, k_out, v_bf, tot, sel, seg_ones, st)


def _chunk_cumsum(g, tri):
    hi, mid, lo = _split3(g)
    w = _dot(tri, hi) + _dot(tri, mid) + _dot(tri, lo)
    d = w - _bcast_block_row(w, CHUNK, CHUNK // 2 - 1)
    bad = jnp.max(jnp.where(jnp.abs(d) <= FAST_BOUND, 0.0, 1.0))
    return w, d, bad


def _linear_recurrence(a, u, carry_ref):
    rows, width = a.shape
    row = _row_ids((rows, width))
    p, h = a, u
    span = 1
    while span < SUBLANES:
        right = (row & span) != 0
        pm = _bcast_block_row(p, 2 * span, span - 1)
        hm = _bcast_block_row(h, 2 * span, span - 1)
        h = jnp.where(right, h + p * hm, h)
        p = jnp.where(right, p * pm, p)
        span *= 2
    carry = carry_ref[...]
    outs = []
    for grp in range(rows // SUBLANES):
        sl = slice(grp * SUBLANES, (grp + 1) * SUBLANES)
        hg = h[sl] + p[sl] * carry
        carry = hg[SUBLANES - 1:SUBLANES, :]
        outs.append(hg)
    carry_ref[...] = carry
    return jnp.concatenate(outs, axis=0)


def _mixer_kernel(layer, x_ref, tri_ref, segones_ref, selc_ref, nmix_ref, win_ref, lbraw_ref,
                  anorm_ref, convw_ref, convb_ref, wa_ref, ba_ref, wx_ref, bx_ref, lam_ref, wup_ref,
                  bg_ref, cnorm_ref, wout_ref, out_ref, sta_ref, stc_ref, hb_ref, xbtail_ref, la_ref):
    rows = x_ref.shape[0]

    @pl.when(pl.program_id(1) == 0)
    def _():
        sta_ref[...] = jnp.zeros_like(sta_ref)
        stc_ref[...] = jnp.zeros_like(stc_ref)
        hb_ref[...] = jnp.zeros_like(hb_ref)
        xbtail_ref[...] = jnp.zeros_like(xbtail_ref)

    x = x_ref[...]
    h = _rmsnorm(x, nmix_ref[...]).astype(BF16)
    proj = _dot(h, win_ref[...])
    tri = tri_ref[...]
    seg_ones = segones_ref[...]

    lbraw = lbraw_ref[...]
    e = jnp.exp(lbraw - jnp.max(lbraw, axis=0, keepdims=True))
    sm = e / jnp.sum(e, axis=0, keepdims=True)
    lb = jnp.sum(sm[0:layer + 1], axis=0, keepdims=True) - sm[0:1]
    z = proj[:, OFF_AF:OFF_AF + 256]
    f = lb + (1.0 - lb) * jax.nn.sigmoid(z)
    log_f = jnp.log(jnp.maximum(f, F_FLOOR))
    key = (1.0 - lb) * jax.nn.sigmoid(-z)
    q_a, v_a = proj[:, OFF_AQ:OFF_AQ + 256], proj[:, OFF_AI:OFF_AI + 256].astype(BF16)
    w_a, d_a, bad_a = _chunk_cumsum(log_f, tri)

    xb = proj[:, OFF_BX:OFF_BX + B_WIDTH]
    tail = xbtail_ref[...]
    xbtail_ref[...] = xb[rows - SUBLANES:rows, :]
    convw = convw_ref[...]
    xc = convb_ref[...] + xb * convw[B_CONV - 1:B_CONV, :]
    row8 = _row_ids((SUBLANES, B_WIDTH))
    for kk in range(B_CONV - 1):
        back = B_CONV - 1 - kk
        sh = pltpu.roll(xb, back, 0)
        top = jnp.where(row8 < back, pltpu.roll(tail, back, 0), sh[0:SUBLANES, :])
        xc = xc + jnp.concatenate([top, sh[SUBLANES:, :]], axis=0) * convw[kk:kk + 1, :]
    xc_bf = xc.astype(BF16)
    r = jax.nn.sigmoid(_dot(xc_bf, wa_ref[...]) + ba_ref[...])
    gi = jax.nn.sigmoid(_dot(xc_bf, wx_ref[...]) + bx_ref[...])
    log_a = (-LRU_C * r) * _softplus(-lam_ref[...])
    a = jnp.exp(log_a)
    one_m_a2 = (1.0 + a * a) * jnp.tanh(-log_a)
    u = jnp.where(one_m_a2 > 0.0, one_m_a2 * lax.rsqrt(one_m_a2), 0.0) * (gi * xc)
    hs = _linear_recurrence(a, u, hb_ref)
    o_b = hs * jax.nn.gelu(proj[:, OFF_BG:OFF_BG + B_WIDTH])

    glr = proj[:, OFF_CLR:OFF_CLR + LANES].astype(BF16)
    gate = _dot(glr, wup_ref[...]) + bg_ref[...]
    log_fc = -_softplus(-gate) / C_GATE_NORM
    q_c, k_c = proj[:, OFF_CQ:OFF_CQ + 128] * (C_DK ** -0.5), proj[:, OFF_CK:OFF_CK + 128]
    v_c = proj[:, OFF_CV:OFF_CV + 256].astype(BF16)
    w_c, d_c, bad_c = _chunk_cumsum(log_fc, tri)
    sel_c = selc_ref[...]

    st_a, st_c = sta_ref[...], stc_ref[...]
    la_ref[0], sta_ref[...] = _la_fast(q_a, key, v_a, w_a, d_a, seg_ones, seg_ones, st_a)
    la_ref[1], stc_ref[...] = _la_fast(q_c, k_c, v_c, w_c, d_c, sel_c, seg_ones, st_c)

    @pl.when(jnp.maximum(bad_a, bad_c) != 0.0)
    def _():
        la_ref[0], sta_ref[...] = _la_safe(q_a, key, v_a, log_f, seg_ones, seg_ones, st_a)
        la_ref[1], stc_ref[...] = _la_safe(q_c, k_c, v_c, log_fc, sel_c, seg_ones, st_c)

    o_a = _head_rmsnorm(la_ref[0], anorm_ref[...], seg_ones) * jax.nn.silu(proj[:, OFF_AG:OFF_AG + 256])
    o_c = _head_rmsnorm(la_ref[1], cnorm_ref[...], seg_ones) * jax.nn.silu(proj[:, OFF_CG:OFF_CG + 256])

    mix = jnp.concatenate([o_a, o_b, o_c], axis=-1).astype(BF16)
    out_ref[...] = x + _dot(mix, wout_ref[...])


def _ffn_kernel(final, x_ref, nffn_ref, wg_ref, wv_ref, convw_ref, convb_ref, wd_ref, nfin_ref,
                out_ref, gext_ref):
    rows = x_ref.shape[0]

    @pl.when(pl.program_id(1) == 0)
    def _():
        gext_ref[0:SUBLANES, :] = jnp.zeros((SUBLANES, D_FF), F32)

    x = x_ref[...]
    h = _rmsnorm(x, nffn_ref[...]).astype(BF16)
    pre = _dot(h, wg_ref[...])
    gext_ref[SUBLANES:SUBLANES + rows, :] = pre
    convw = convw_ref[...]
    gate = convb_ref[...] + pre * convw[FFN_CONV - 1:FFN_CONV, :]
    for kk in range(FFN_CONV - 1):
        back = FFN_CONV - 1 - kk
        gate = gate + gext_ref[pl.ds(SUBLANES - back, rows), :] * convw[kk:kk + 1, :]
    gext_ref[0:SUBLANES, :] = gext_ref[rows:rows + SUBLANES, :]
    val = _dot(h, wv_ref[...])
    act = (jax.nn.gelu(gate) * val).astype(BF16)
    y = x + _dot(act, wd_ref[...])
    if final:
        y = _rmsnorm(y, nfin_ref[...])
    out_ref[...] = y


def _const_spec(shape):
    return pl.BlockSpec(shape, lambda b, t: (0,) * len(shape), pipeline_mode=pl.Buffered(1))


def _tile_spec(rows):
    return pl.BlockSpec((None, rows, D_MODEL), lambda b, t: (b, t, 0))


def _compiler_params():
    return pltpu.CompilerParams(dimension_semantics=("arbitrary", "arbitrary"),
                                vmem_limit_bytes=VMEM_LIMIT)


def _selectors(rows):
    i = np.arange(rows)
    tri = ((i[:, None] // CHUNK == i[None, :] // CHUNK) & (i[None, :] <= i[:, None]))
    r = np.arange(V_WIDTH) // HEAD_DIM
    seg_ones = r[:, None] == (np.arange(V_WIDTH) // HEAD_DIM)[None, :]
    sel_c = r[:, None] == (np.arange(N_HEADS * C_DK) // C_DK)[None, :]
    return [jnp.asarray(m, dtype=BF16) for m in (tri, seg_ones, sel_c)]


def _mixer_call(layer, x, consts):
    batch, seq, _ = x.shape
    rows = min(MIX_ROWS, seq)
    consts = _selectors(rows) + consts
    return pl.pallas_call(
        functools.partial(_mixer_kernel, layer),
        grid=(batch, seq // rows),
        in_specs=[_tile_spec(rows)] + [_const_spec(c.shape) for c in consts],
        out_specs=_tile_spec(rows),
        out_shape=jax.ShapeDtypeStruct(x.shape, F32),
        scratch_shapes=[pltpu.VMEM((V_WIDTH, N_HEADS * A_DK), F32),
                        pltpu.VMEM((V_WIDTH, N_HEADS * C_DK), F32),
                        pltpu.VMEM((1, B_WIDTH), F32),
                        pltpu.VMEM((SUBLANES, B_WIDTH), F32),
                        pltpu.VMEM((2, rows, V_WIDTH), F32)],
        compiler_params=_compiler_params(),
        name=f"mixer{layer}",
    )(x, *consts)


def _ffn_call(layer, final, x, consts):
    batch, seq, _ = x.shape
    rows = min(FFN_ROWS, seq)
    return pl.pallas_call(
        functools.partial(_ffn_kernel, final),
        grid=(batch, seq // rows),
        in_specs=[_tile_spec(rows)] + [_const_spec(c.shape) for c in consts],
        out_specs=_tile_spec(rows),
        out_shape=jax.ShapeDtypeStruct(x.shape, F32),
        scratch_shapes=[pltpu.VMEM((rows + SUBLANES, D_FF), F32)],
        compiler_params=_compiler_params(),
        name=f"ffn{layer}",
    )(x, *consts)


def _block_diag(w):
    nb, n, _ = w.shape
    eye = jnp.eye(nb, dtype=w.dtype)
    return (eye[:, None, :, None] * w[:, :, None, :]).reshape(nb * n, nb * n)


def kernel(x, norm_mix, w_in, hgrn_lower_bounds, hgrn_norm, rg_conv_w, rg_conv_b, rg_w_a, rg_b_a, rg_w_x, rg_b_x, rg_lambda, gla_w_gate_up, gla_b_gate, gla_norm, w_out, norm_ffn, ffn_w_gate, ffn_w_val, ffn_conv_w, ffn_conv_b, ffn_w_down, norm_final):
    depth = w_in.shape[0]
    row = lambda v: v.reshape(1, -1).astype(F32)
    for l in range(depth):
        w_in_pad = jnp.pad(w_in[l], ((0, 0), (0, IN_COLS_PAD - IN_COLS))).astype(BF16)
        w_up_pad = jnp.pad(gla_w_gate_up[l], ((0, LANES - C_GATE_RANK), (0, 0))).astype(BF16)
        mixer_consts = [
            row(norm_mix[l]), w_in_pad, hgrn_lower_bounds.astype(F32), row(hgrn_norm[l]),
            rg_conv_w[l].astype(F32), row(rg_conv_b[l]),
            _block_diag(rg_w_a[l]).astype(BF16), row(rg_b_a[l]),
            _block_diag(rg_w_x[l]).astype(BF16), row(rg_b_x[l]), row(rg_lambda[l]),
            w_up_pad, row(gla_b_gate[l]), row(gla_norm[l]), w_out[l].astype(BF16),
        ]
        x = _mixer_call(l, x, mixer_consts)
        ffn_consts = [
            row(norm_ffn[l]), ffn_w_gate[l].astype(BF16), ffn_w_val[l].astype(BF16),
            ffn_conv_w[l].astype(F32), row(ffn_conv_b[l]), ffn_w_down[l].astype(BF16),
            row(norm_final),
        ]
        x = _ffn_call(l, l == depth - 1, x, ffn_consts)
    return x
```

```python
import functools

import numpy as np
import jax
import jax.numpy as jnp
from jax import lax
from jax.experimental import pallas as pl
from jax.experimental.pallas import tpu as pltpu

D_MODEL = 1024
HEAD_DIM = 64
N_HEADS = 4
A_DK = 64
C_DK = 32
B_WIDTH = 512
B_CONV = 4
LRU_C = 8.0
C_GATE_RANK = 16
C_GATE_NORM = 16.0
D_FF = 2688
FFN_CONV = 3
EPS = 1e-6
F_FLOOR = 1e-20
V_WIDTH = N_HEADS * HEAD_DIM

OFF_AQ, OFF_AF, OFF_AI, OFF_AG = 0, 256, 512, 768
OFF_BX, OFF_BG = 1024, 1536
OFF_CQ, OFF_CK, OFF_CV, OFF_CG, OFF_CLR = 2048, 2176, 2304, 2560, 2816
IN_COLS = 2832
LANES = 128
SUBLANES = 8
IN_COLS_PAD = 2944

CHUNK = 64
MIX_ROWS = 256
FFN_ROWS = 512
VMEM_LIMIT = 52 * 1024 * 1024
FAST_BOUND = 60.0
SEG_ROWS = MIX_ROWS // SUBLANES
SEG_PITCH = SEG_ROWS + SUBLANES
N_SLABS = B_WIDTH // LANES

BF16 = jnp.bfloat16
F32 = jnp.float32


def _dot(a, b):
    return jnp.dot(a, b, preferred_element_type=F32)


def _dot_nt(a, b):
    return lax.dot_general(a, b, (((1,), (1,)), ((), ())), preferred_element_type=F32)


def _dot_tn(a, b):
    return lax.dot_general(a, b, (((0,), (0,)), ((), ())), preferred_element_type=F32)


def _rmsnorm(x, w):
    return x * lax.rsqrt(jnp.mean(x * x, axis=-1, keepdims=True) + EPS) * w


def _softplus(x):
    return jnp.maximum(x, 0.0) + jnp.log1p(jnp.exp(-jnp.abs(x)))


def _row_ids(shape):
    return lax.broadcasted_iota(jnp.int32, shape, 0)


def _bcast_block_row(x, blk, r):
    n, w = x.shape
    if blk >= SUBLANES:
        row = x.reshape(n // blk, blk, w)[:, r:r + 1, :]
        return jnp.broadcast_to(row, (n // blk, blk, w)).reshape(n, w)
    rid = _row_ids((n, w)) & (blk - 1)
    out = x
    for m in range(blk):
        if m != r:
            out = jnp.where(rid == m, pltpu.roll(x, (m - r) % n, 0), out)
    return out


def _split2(y):
    hi = y.astype(BF16)
    lo = (y - hi.astype(F32)).astype(BF16)
    return hi, lo


def _split3(y):
    hi = y.astype(BF16)
    r1 = y - hi.astype(F32)
    mid = r1.astype(BF16)
    lo = (r1 - mid.astype(F32)).astype(BF16)
    return hi, mid, lo


def _head_rmsnorm(o, w, seg_ones):
    hi, lo = _split2(o * o)
    ms = (_dot(hi, seg_ones) + _dot(lo, seg_ones)) * (1.0 / HEAD_DIM)
    return o * lax.rsqrt(ms + EPS) * w


class _ChunkedAttention:
    def __init__(self, levels, level_masks, q_in, k_out, v_bf, tot, sel, seg_ones, st):
        self.levels, self.level_masks = levels, level_masks
        self.q_in, self.k_out, self.v_bf, self.tot = q_in, k_out, v_bf, tot
        self.sel, self.seg_ones, self.st = sel, seg_ones, st
        self.n_chunks = v_bf.shape[0] // CHUNK
        self.outs = []

    def step(self):
        c = len(self.outs)
        sl = slice(c * CHUNK, (c + 1) * CHUNK)
        scores = None
        for (qh, kh), msk in zip(self.levels, self.level_masks):
            k_rep = jnp.concatenate([kh[sl]] * N_HEADS, axis=0) * self.sel
            s = _dot_nt(qh[sl], k_rep)
            scores = jnp.where(msk, s, 0.0 if scores is None else scores)
        v_rep = jnp.concatenate([self.v_bf[sl]] * N_HEADS, axis=0) * self.seg_ones
        o = _dot(scores.astype(BF16), v_rep)
        self.outs.append(o + _dot_nt(self.q_in[sl], self.st.astype(BF16) * self.sel))
        self.st = (self.st * jnp.exp(self.tot[c * CHUNK:c * CHUNK + 1, :])
                   + _dot_tn(self.v_bf[sl], self.k_out[sl]))

    def result(self):
        while len(self.outs) < self.n_chunks:
            self.step()
        return jnp.concatenate(self.outs, axis=0), self.st


def _chunk_ids():
    ii = _row_ids((CHUNK, V_WIDTH))
    jj = lax.broadcasted_iota(jnp.int32, (CHUNK, V_WIDTH), 1) & (CHUNK - 1)
    return ii, jj


def _la_fast(q, k, v_bf, w, d, sel, seg_ones, st):
    tot = _bcast_block_row(w, CHUNK, CHUNK - 1)
    q_hat = (q * jnp.exp(d)).astype(BF16)
    k_hat = (k * jnp.exp(-d)).astype(BF16)
    q_in = (q * jnp.exp(w)).astype(BF16)
    k_out = (k * jnp.exp(tot - w)).astype(BF16)
    ii, jj = _chunk_ids()
    return _ChunkedAttention([(q_hat, k_hat)], [jj <= ii], q_in, k_out, v_bf, tot, sel, seg_ones, st)


def _la_safe(q, k, v_bf, g, sel, seg_ones, st):
    rows, hk = q.shape
    row = _row_ids((rows, hk))
    ii, jj = _chunk_ids()
    levels = [(q.astype(BF16), k.astype(BF16))]
    level_masks = [ii == jj]
    w = g
    span = 1
    while span < CHUNK:
        mid = _bcast_block_row(w, 2 * span, span - 1)
        right = (row & span) != 0
        qh = jnp.where(right, q * jnp.exp(jnp.where(right, w, 0.0)), 0.0)
        kh = jnp.where(right, 0.0, k * jnp.exp(jnp.where(right, 0.0, mid - w)))
        levels.append((qh.astype(BF16), kh.astype(BF16)))
        shift = span.bit_length()
        level_masks.append((ii >> shift) == (jj >> shift))
        w = w + jnp.where(right, mid, 0.0)
        span *= 2
    tot = _bcast_block_row(w, CHUNK, CHUNK - 1)
    q_in = (q * jnp.exp(w)).astype(BF16)
    k_out = (k * jnp.exp(tot - w)).astype(BF16)
    return _ChunkedAttention(levels[::-1], level_masks[::-1], q_in, k_out, v_bf, tot, sel, seg_ones, st)


def _chunk_cumsum(g, tri):
    hi, mid, lo = _split3(g)
    w = _dot(tri, hi) + _dot(tri, mid) + _dot(tri, lo)
    d = w - _bcast_block_row(w, CHUNK, CHUNK // 2 - 1)
    bad = jnp.max(jnp.where(jnp.abs(d) <= FAST_BOUND, 0.0, 1.0))
    return w, d, bad


def _linear_recurrence(a, u, carry_ref, scan_ref):
    rows = a.shape[0]
    seg_rows = rows // SUBLANES
    for s in range(SUBLANES):
        scan_ref[0, s * SEG_PITCH:s * SEG_PITCH + seg_rows, :] = a[s * seg_rows:(s + 1) * seg_rows, :]
        scan_ref[1, s * SEG_PITCH:s * SEG_PITCH + seg_rows, :] = u[s * seg_rows:(s + 1) * seg_rows, :]
    h = p = None
    for t in range(seg_rows):
        rows_t = pl.ds(t, SUBLANES, stride=SEG_PITCH)
        a_t, u_t = scan_ref[0, rows_t, :], scan_ref[1, rows_t, :]
        h = u_t if t == 0 else a_t * h + u_t
        p = a_t if t == 0 else a_t * p
        scan_ref[2, rows_t, :] = h
        scan_ref[3, rows_t, :] = p
    carry = carry_ref[...]
    outs = []
    for s in range(SUBLANES):
        seg = slice(s * SEG_PITCH, s * SEG_PITCH + seg_rows)
        hs = scan_ref[2, seg, :] + scan_ref[3, seg, :] * carry
        carry = hs[seg_rows - 1:seg_rows, :]
        outs.append(hs)
    carry_ref[...] = carry
    return jnp.concatenate(outs, axis=0)


def _mixer_kernel(layer, tiles_per_seq, xn_ref, xr_ref, tri_ref, segones_ref, selc_ref, nmix_ref,
                  win_ref, lbraw_ref, anorm_ref, convw_ref, convb_ref, wa_ref, ba_ref, wx_ref, bx_ref,
                  lam_ref, wup_ref, bg_ref, cnorm_ref, wout_ref, out_ref, sta_ref, stc_ref, hb_ref,
                  xbtail_ref, proj_ref, mix_ref, scan_ref):
    rows = xn_ref.shape[0]
    step = pl.program_id(0)
    mix_tile = jnp.maximum(step - 1, 0)

    @pl.when(mix_tile % tiles_per_seq == 0)
    def _():
        sta_ref[...] = jnp.zeros_like(sta_ref)
        stc_ref[...] = jnp.zeros_like(stc_ref)
        hb_ref[...] = jnp.zeros_like(hb_ref)
        xbtail_ref[...] = jnp.zeros_like(xbtail_ref)

    @pl.when(step == 0)
    def _():
        proj_ref[...] = jnp.zeros_like(proj_ref)
        mix_ref[...] = jnp.zeros_like(mix_ref)

    h_next = _rmsnorm(xn_ref[...], nmix_ref[...]).astype(BF16)

    def project(lo, hi):
        proj_ref[:, lo:hi] = _dot(h_next, win_ref[:, lo:hi])

    def proj(off, width):
        return proj_ref[:, off:off + width]

    out_ref[...] = xr_ref[...] + _dot(mix_ref[...], wout_ref[...])

    tri = tri_ref[...]
    seg_ones = segones_ref[...]

    lbraw = lbraw_ref[...]
    e = jnp.exp(lbraw - jnp.max(lbraw, axis=0, keepdims=True))
    sm = e / jnp.sum(e, axis=0, keepdims=True)
    lb = jnp.sum(sm[0:layer + 1], axis=0, keepdims=True) - sm[0:1]
    z = proj(OFF_AF, 256)
    f = lb + (1.0 - lb) * jax.nn.sigmoid(z)
    log_f = jnp.log(jnp.maximum(f, F_FLOOR))
    key = (1.0 - lb) * jax.nn.sigmoid(-z)
    q_a, v_a = proj(OFF_AQ, 256), proj(OFF_AI, 256).astype(BF16)
    w_a, d_a, bad_a = _chunk_cumsum(log_f, tri)
    gate_a = jax.nn.silu(proj(OFF_AG, 256))

    glr = proj(OFF_CLR, LANES).astype(BF16)
    gate = _dot(glr, wup_ref[...]) + bg_ref[...]
    log_fc = -_softplus(-gate) / C_GATE_NORM
    q_c, k_c = proj(OFF_CQ, 128) * (C_DK ** -0.5), proj(OFF_CK, 128)
    v_c = proj(OFF_CV, 256).astype(BF16)
    w_c, d_c, bad_c = _chunk_cumsum(log_fc, tri)
    sel_c = selc_ref[...]
    gate_c = jax.nn.silu(proj(OFF_CG, 256))

    def rglru_slab(j):
        cols = slice(j * LANES, (j + 1) * LANES)
        xb = proj(OFF_BX + j * LANES, LANES)
        gate_b = jax.nn.gelu(proj(OFF_BG + j * LANES, LANES))
        tail = xbtail_ref[:, cols]
        xbtail_ref[:, cols] = xb[rows - SUBLANES:rows, :]
        convw = convw_ref[:, cols]
        xc = convb_ref[:, cols] + xb * convw[B_CONV - 1:B_CONV, :]
        row8 = _row_ids((SUBLANES, LANES))
        for kk in range(B_CONV - 1):
            back = B_CONV - 1 - kk
            sh = pltpu.roll(xb, back, 0)
            top = jnp.where(row8 < back, pltpu.roll(tail, back, 0), sh[0:SUBLANES, :])
            xc = xc + jnp.concatenate([top, sh[SUBLANES:, :]], axis=0) * convw[kk:kk + 1, :]
        xc_bf = xc.astype(BF16)
        r = jax.nn.sigmoid(_dot(xc_bf, wa_ref[cols, cols]) + ba_ref[:, cols])
        gi = jax.nn.sigmoid(_dot(xc_bf, wx_ref[cols, cols]) + bx_ref[:, cols])
        log_a = (-LRU_C * r) * _softplus(-lam_ref[:, cols])
        a = jnp.exp(log_a)
        one_m_a2 = (1.0 + a * a) * jnp.tanh(-log_a)
        u = jnp.where(one_m_a2 > 0.0, one_m_a2 * lax.rsqrt(one_m_a2), 0.0) * (gi * xc)
        hs = _linear_recurrence(a, u, hb_ref.at[:, cols], scan_ref.at[j])
        mix_ref[:, V_WIDTH + j * LANES:V_WIDTH + (j + 1) * LANES] = (hs * gate_b).astype(BF16)

    st_a, st_c = sta_ref[...], stc_ref[...]
    att_a = _la_fast(q_a, key, v_a, w_a, d_a, seg_ones, seg_ones, st_a)
    att_c = _la_fast(q_c, k_c, v_c, w_c, d_c, sel_c, seg_ones, st_c)
    pieces = [(OFF_AQ, OFF_AG), (OFF_AG, OFF_BX), (OFF_CQ, OFF_CG), (OFF_CG, IN_COLS_PAD)]
    for j in range(N_SLABS):
        if j < att_a.n_chunks:
            att_a.step()
            att_c.step()
        rglru_slab(j)
        project(*pieces[j])
    project(OFF_BX, OFF_CQ)

    def finish(att_a, att_c):
        o_a, sta_ref[...] = att_a.result()
        o_c, stc_ref[...] = att_c.result()
        mix_ref[:, 0:V_WIDTH] = (_head_rmsnorm(o_a, anorm_ref[...], seg_ones) * gate_a).astype(BF16)
        mix_ref[:, V_WIDTH + B_WIDTH:] = (
            _head_rmsnorm(o_c, cnorm_ref[...], seg_ones) * gate_c).astype(BF16)

    finish(att_a, att_c)

    @pl.when(jnp.maximum(bad_a, bad_c) != 0.0)
    def _():
        finish(_la_safe(q_a, key, v_a, log_f, seg_ones, seg_ones, st_a),
               _la_safe(q_c, k_c, v_c, log_fc, sel_c, seg_ones, st_c))


def _ffn_kernel(final, x_ref, nffn_ref, wg_ref, wv_ref, convw_ref, convb_ref, wd_ref, nfin_ref,
                out_ref, gext_ref):
    rows = x_ref.shape[0]

    @pl.when(pl.program_id(1) == 0)
    def _():
        gext_ref[0:SUBLANES, :] = jnp.zeros((SUBLANES, D_FF), F32)

    x = x_ref[...]
    h = _rmsnorm(x, nffn_ref[...]).astype(BF16)
    pre = _dot(h, wg_ref[...])
    gext_ref[SUBLANES:SUBLANES + rows, :] = pre
    convw = convw_ref[...]
    gate = convb_ref[...] + pre * convw[FFN_CONV - 1:FFN_CONV, :]
    for kk in range(FFN_CONV - 1):
        back = FFN_CONV - 1 - kk
        gate = gate + gext_ref[pl.ds(SUBLANES - back, rows), :] * convw[kk:kk + 1, :]
    gext_ref[0:SUBLANES, :] = gext_ref[rows:rows + SUBLANES, :]
    val = _dot(h, wv_ref[...])
    act = (jax.nn.gelu(gate) * val).astype(BF16)
    y = x + _dot(act, wd_ref[...])
    if final:
        y = _rmsnorm(y, nfin_ref[...])
    out_ref[...] = y


def _selectors(rows):
    i = np.arange(rows)
    tri = ((i[:, None] // CHUNK == i[None, :] // CHUNK) & (i[None, :] <= i[:, None]))
    r = np.arange(V_WIDTH) // HEAD_DIM
    seg_ones = r[:, None] == (np.arange(V_WIDTH) // HEAD_DIM)[None, :]
    sel_c = r[:, None] == (np.arange(N_HEADS * C_DK) // C_DK)[None, :]
    return [jnp.asarray(m, dtype=BF16) for m in (tri, seg_ones, sel_c)]


def _mixer_call(layer, x, consts):
    batch, seq, _ = x.shape
    assert seq % MIX_ROWS == 0
    rows = MIX_ROWS
    tiles_per_seq = seq // rows
    n_tiles = batch * tiles_per_seq
    consts = _selectors(rows) + consts

    def tile_spec(offset):
        def index_map(s):
            t = jnp.clip(s - offset, 0, n_tiles - 1)
            return t // tiles_per_seq, t % tiles_per_seq, 0
        return pl.BlockSpec((None, rows, D_MODEL), index_map)

    const_spec = lambda shape: pl.BlockSpec(shape, lambda s: (0,) * len(shape),
                                            pipeline_mode=pl.Buffered(1))
    return pl.pallas_call(
        functools.partial(_mixer_kernel, layer, tiles_per_seq),
        grid=(n_tiles + 2,),
        in_specs=[tile_spec(0), tile_spec(2)] + [const_spec(c.shape) for c in consts],
        out_specs=tile_spec(2),
        out_shape=jax.ShapeDtypeStruct(x.shape, F32),
        scratch_shapes=[pltpu.VMEM((V_WIDTH, N_HEADS * A_DK), F32),
                        pltpu.VMEM((V_WIDTH, N_HEADS * C_DK), F32),
                        pltpu.VMEM((1, B_WIDTH), F32),
                        pltpu.VMEM((SUBLANES, B_WIDTH), F32),
                        pltpu.VMEM((rows, IN_COLS_PAD), F32),
                        pltpu.VMEM((rows, D_MODEL), BF16),
                        pltpu.VMEM((N_SLABS, 4, SUBLANES * SEG_PITCH, LANES), F32)],
        compiler_params=pltpu.CompilerParams(dimension_semantics=("arbitrary",),
                                             vmem_limit_bytes=VMEM_LIMIT),
        name=f"mixer{layer}",
    )(x, x, *consts)


def _ffn_call(layer, final, x, consts):
    batch, seq, _ = x.shape
    rows = min(FFN_ROWS, seq)
    tile_spec = pl.BlockSpec((None, rows, D_MODEL), lambda b, t: (b, t, 0))
    const_spec = lambda shape: pl.BlockSpec(shape, lambda b, t: (0,) * len(shape),
                                            pipeline_mode=pl.Buffered(1))
    return pl.pallas_call(
        functools.partial(_ffn_kernel, final),
        grid=(batch, seq // rows),
        in_specs=[tile_spec] + [const_spec(c.shape) for c in consts],
        out_specs=tile_spec,
        out_shape=jax.ShapeDtypeStruct(x.shape, F32),
        scratch_shapes=[pltpu.VMEM((rows + SUBLANES, D_FF), F32)],
        compiler_params=pltpu.CompilerParams(dimension_semantics=("arbitrary", "arbitrary"),
                                             vmem_limit_bytes=VMEM_LIMIT),
        name=f"ffn{layer}",
    )(x, *consts)


def _block_diag(w):
    nb, n, _ = w.shape
    eye = jnp.eye(nb, dtype=w.dtype)
    return (eye[:, None, :, None] * w[:, :, None, :]).reshape(nb * n, nb * n)


def kernel(x, norm_mix, w_in, hgrn_lower_bounds, hgrn_norm, rg_conv_w, rg_conv_b, rg_w_a, rg_b_a, rg_w_x, rg_b_x, rg_lambda, gla_w_gate_up, gla_b_gate, gla_norm, w_out, norm_ffn, ffn_w_gate, ffn_w_val, ffn_conv_w, ffn_conv_b, ffn_w_down, norm_final):
    depth = w_in.shape[0]
    row = lambda v: v.reshape(1, -1).astype(F32)
    for l in range(depth):
        w_in_pad = jnp.pad(w_in[l], ((0, 0), (0, IN_COLS_PAD - IN_COLS))).astype(BF16)
        w_up_pad = jnp.pad(gla_w_gate_up[l], ((0, LANES - C_GATE_RANK), (0, 0))).astype(BF16)
        mixer_consts = [
            row(norm_mix[l]), w_in_pad, hgrn_lower_bounds.astype(F32), row(hgrn_norm[l]),
            rg_conv_w[l].astype(F32), row(rg_conv_b[l]),
            _block_diag(rg_w_a[l]).astype(BF16), row(rg_b_a[l]),
            _block_diag(rg_w_x[l]).astype(BF16), row(rg_b_x[l]), row(rg_lambda[l]),
            w_up_pad, row(gla_b_gate[l]), row(gla_norm[l]), w_out[l].astype(BF16),
        ]
        x = _mixer_call(l, x, mixer_consts)
        ffn_consts = [
            row(norm_ffn[l]), ffn_w_gate[l].astype(BF16), ffn_w_val[l].astype(BF16),
            ffn_conv_w[l].astype(F32), row(ffn_conv_b[l]), ffn_w_down[l].astype(BF16),
            row(norm_final),
        ]
        x = _ffn_call(l, l == depth - 1, x, ffn_consts)
    return x
```
